```python
import jax
import jax.numpy as jnp
from jax import lax
import numpy as np

D_MODEL = 1024
BATCH = 2
SEQ = 8192
DEPTH = 1
DEC_BATCH = 128
DEC_SEQ = 4
PAST_LEN = 2048
PAGE_SIZE = 128

HEAD_DIM = 64
MOBA_HEADS = 8
NSA_HEADS = 8
NSA_KV_HEADS = 2
NSA_GROUP = NSA_HEADS // NSA_KV_HEADS
MOBA_BLOCK = 256
MOBA_TOPK = 3
CMP_LEN = 32
CMP_STRIDE = 16
CMP_HIDDEN = 4 * HEAD_DIM
SEL_BLOCK = 64
SEL_TOPK = 16
WINDOW = 512
D_FF = 4 * D_MODEL
Q_BLOCK = 64
EPS = 1e-6
SEL_FORCE = 1e9
MASK_NEG = -1e30
MOBA_W = MOBA_HEADS * HEAD_DIM
NSA_W = NSA_HEADS * HEAD_DIM
KV_W = NSA_KV_HEADS * HEAD_DIM
N_GATES = 3 * NSA_HEADS
MIX_W = MOBA_W + NSA_W
PROJ_W = 3 * MOBA_W + NSA_W + 6 * KV_W + N_GATES

kernel_name = 'hybrid_moba_nsa_decoder_step'


def _rms_norm(x, g):
    xf = x.astype(jnp.float32)
    y = xf * lax.rsqrt(jnp.mean(xf * xf, axis=-1, keepdims=True) + EPS)
    return (y * g.astype(jnp.float32)).astype(x.dtype)


def _alibi_slopes():
    n = MOBA_HEADS + NSA_HEADS
    m = 2.0 ** (-8.0 * jnp.arange(1, n + 1, dtype=jnp.float32) / n)
    return m[0::2][:MOBA_HEADS], m[1::2][:NSA_HEADS]


def _to_blocks(k, block, min_blocks):
    b, l, h, d = k.shape
    nb = max(-(-l // block), min_blocks)
    k = jnp.pad(k, ((0, 0), (0, nb * block - l), (0, 0), (0, 0)))
    return k.reshape(b, nb, block, h, d).transpose(0, 3, 1, 2, 4)


def _gather_blocks(blocks, idx):
    return jax.vmap(jax.vmap(lambda a, i: a[i]))(blocks, idx)


def _gather_pages(pool, page_table):
    g = pool[page_table]
    return g.reshape((g.shape[0], g.shape[1] * g.shape[2]) + pool.shape[2:])


def _compress(k, pos, w1, b1, w2):
    b, l, h, d = k.shape
    lp = max(l, CMP_LEN)
    k = jnp.pad(k, ((0, 0), (0, lp - l), (0, 0), (0, 0)))
    nc = (lp - CMP_LEN) // CMP_STRIDE + 1
    idx = jnp.arange(nc)[:, None] * CMP_STRIDE + jnp.arange(CMP_LEN)[None, :]
    win = k[:, idx] + pos[None, None, :, None, :]
    hid = jax.nn.gelu(jnp.einsum('bnlhd,ldf->bnhf', win, w1) + b1)
    return jnp.einsum('bnhf,fd->bnhd', hid, w2)


def _sel_overlap(nc, ns):
    start = jnp.arange(nc)[:, None] * CMP_STRIDE
    blk = jnp.arange(ns)[None, :] * SEL_BLOCK
    return ((start <= blk + SEL_BLOCK - 1) & (start + CMP_LEN - 1 >= blk)).astype(jnp.float32)


def _moba_attend(q, q_pos, kb, vb, kmean, slopes):
    b, tq, h, d = q.shape
    nb, blk = kb.shape[2], kb.shape[3]
    qt = q.transpose(0, 2, 1, 3)
    gate = jnp.einsum('bhtd,bhnd->bhtn', qt.astype(jnp.float32), kmean)
    own = q_pos // MOBA_BLOCK
    past = jnp.arange(nb)[None, :] < own[:, None]
    gate = jnp.where(past, gate, -jnp.inf)
    top_val, top_idx = lax.top_k(gate, MOBA_TOPK)
    idx = jnp.concatenate([top_idx.astype(jnp.int32),
                           jnp.broadcast_to(own[None, None, :, None], (b, h, tq, 1)).astype(jnp.int32)], axis=-1)
    ok = jnp.concatenate([jnp.isfinite(top_val), jnp.ones((b, h, tq, 1), bool)], axis=-1)
    kg = _gather_blocks(kb, idx)
    vg = _gather_blocks(vb, idx)
    kpos = idx[..., None] * blk + jnp.arange(blk)
    dist = (q_pos[None, None, :, None, None] - kpos).astype(jnp.float32)
    mask = ok[..., None] & (dist >= 0)
    s = jnp.einsum('bhtd,bhtjmd->bhtjm', qt, kg, preferred_element_type=jnp.float32) * (d ** -0.5)
    s = s - slopes[None, :, None, None, None] * dist
    s = jnp.where(mask, s, -jnp.inf).reshape(b, h, tq, -1)
    p = jax.nn.softmax(s, axis=-1).astype(vg.dtype)
    return jnp.einsum('bhtn,bhtnd->bthd', p, vg.reshape(b, h, tq, -1, d))


def _nsa_cmp_attend(q, q_pos, kc, vc, c_end, slopes):
    b, tq, h, d = q.shape
    qg = q.reshape(b, tq, NSA_KV_HEADS, NSA_GROUP, d)
    sg = slopes.reshape(NSA_KV_HEADS, NSA_GROUP)
    dist = (q_pos[:, None] - c_end[None, :]).astype(jnp.float32)
    mask = (dist >= 0)[None, :, None, None, :]
    s = jnp.einsum('btkgd,bnkd->btkgn', qg, kc, preferred_element_type=jnp.float32) * (d ** -0.5)
    s = s - sg[None, None, :, :, None] * dist[None, :, None, None, :]
    s = jnp.where(mask, s, MASK_NEG)
    p = jax.nn.softmax(s, axis=-1) * mask
    o = jnp.einsum('btkgn,bnkd->btkgd', p.astype(vc.dtype), vc).reshape(b, tq, h, d)
    return o, p


def _nsa_sel_attend(q, q_pos, p_cmp, overlap, kbs, vbs, slopes):
    b, tq, h, d = q.shape
    ns, blk = kbs.shape[2], kbs.shape[3]
    score = jnp.einsum('btkgn,ns->bkts', p_cmp, overlap)
    j = jnp.arange(ns)[None, :]
    own = (q_pos // SEL_BLOCK)[:, None]
    forced = (j == 0) | (j == own) | (j == own - 1)
    score = jnp.where(forced, SEL_FORCE, score)
    score = jnp.where(j <= own, score, -jnp.inf)
    top_val, idx = lax.top_k(score, SEL_TOPK)
    idx = idx.astype(jnp.int32)
    ok = jnp.isfinite(top_val)
    kg = _gather_blocks(kbs, idx)
    vg = _gather_blocks(vbs, idx)
    kpos = idx[..., None] * blk + jnp.arange(blk)
    dist = (q_pos[None, None, :, None, None] - kpos).astype(jnp.float32)
    mask = ok[..., None] & (dist >= 0)
    qg = q.reshape(b, tq, NSA_KV_HEADS, NSA_GROUP, d).transpose(0, 2, 1, 3, 4)
    sg = slopes.reshape(NSA_KV_HEADS, NSA_GROUP)
    s = jnp.einsum('bktgd,bktjmd->bktgjm', qg, kg, preferred_element_type=jnp.float32) * (d ** -0.5)
    s = s - sg[None, :, None, :, None, None] * dist[:, :, :, None]
    s = jnp.where(mask[:, :, :, None], s, -jnp.inf).reshape(b, NSA_KV_HEADS, tq, NSA_GROUP, -1)
    p = jax.nn.softmax(s, axis=-1).astype(vg.dtype)
    o = jnp.einsum('bktgn,bktnd->btkgd', p, vg.reshape(b, NSA_KV_HEADS, tq, -1, d))
    return o.reshape(b, tq, h, d)


def _nsa_win_attend(q, q_pos, kw, vw, k_pos, slopes):
    b, tq, h, d = q.shape
    qg = q.reshape(b, tq, NSA_KV_HEADS, NSA_GROUP, d)
    sg = slopes.reshape(NSA_KV_HEADS, NSA_GROUP)
    dist = (q_pos[:, None] - k_pos[None, :]).astype(jnp.float32)
    mask = (dist >= 0) & (dist < WINDOW) & (k_pos >= 0)[None, :]
    s = jnp.einsum('btkgd,bskd->btkgs', qg, kw, preferred_element_type=jnp.float32) * (d ** -0.5)
    s = s - sg[None, None, :, :, None] * dist[None, :, None, None, :]
    s = jnp.where(mask[None, :, None, None, :], s, -jnp.inf)
    p = jax.nn.softmax(s, axis=-1).astype(vw.dtype)
    return jnp.einsum('btkgs,bskd->btkgd', p, vw).reshape(b, tq, h, d)


def _layer(x, c, past_moba, past_cmp, past_sel, win_buf, pos0, q_block, win_out_len,
           w_ada, b_ada, g_pre_mix, g_post_mix, g_pre_mlp, g_post_mlp, w_in,
           cmp_pos, cmp_w1, cmp_b1, cmp_w2, w_out, w_up, w_down):
    b, t, _ = x.shape
    slopes_m, slopes_n = _alibi_slopes()
    mod = (jax.nn.silu(c) @ w_ada + b_ada)[:, None, :]
    sh_a, sc_a, ga_a, sh_m, sc_m, ga_m = jnp.split(mod, 6, axis=-1)
    h = _rms_norm(x, g_pre_mix) * (1.0 + sc_a) + sh_a
    proj = h @ w_in
    sizes = [MOBA_W] * 3 + [NSA_W] + [KV_W] * 6 + [N_GATES]
    cuts = [int(v) for v in np.cumsum(sizes)[:-1]]
    q_m, k_m, v_m, q_n, kc_n, vc_n, ks_n, vs_n, kw_n, vw_n, gates = jnp.split(proj, cuts, axis=-1)
    q_m, k_m, v_m = [a.reshape(b, t, MOBA_HEADS, HEAD_DIM) for a in (q_m, k_m, v_m)]
    q_n = q_n.reshape(b, t, NSA_HEADS, HEAD_DIM)
    kc_n, vc_n, ks_n, vs_n, kw_n, vw_n = [a.reshape(b, t, NSA_KV_HEADS, HEAD_DIM)
                                          for a in (kc_n, vc_n, ks_n, vs_n, kw_n, vw_n)]
    k_m_all = jnp.concatenate([past_moba[:, :, 0], k_m], axis=1)
    v_m_all = jnp.concatenate([past_moba[:, :, 1], v_m], axis=1)
    kc_all = jnp.concatenate([past_cmp[:, :, 0], kc_n], axis=1)
    vc_all = jnp.concatenate([past_cmp[:, :, 1], vc_n], axis=1)
    ks_all = jnp.concatenate([past_sel[:, :, 0], ks_n], axis=1)
    vs_all = jnp.concatenate([past_sel[:, :, 1], vs_n], axis=1)
    kb_m = _to_blocks(k_m_all, MOBA_BLOCK, MOBA_TOPK)
    vb_m = _to_blocks(v_m_all, MOBA_BLOCK, MOBA_TOPK)
    kmean = jnp.mean(kb_m.astype(jnp.float32), axis=3)
    kc = _compress(kc_all, cmp_pos[0], cmp_w1[0], cmp_b1[0], cmp_w2[0])
    vc = _compress(vc_all, cmp_pos[1], cmp_w1[1], cmp_b1[1], cmp_w2[1])
    c_end = jnp.arange(kc.shape[1]) * CMP_STRIDE + CMP_LEN - 1
    kbs = _to_blocks(ks_all, SEL_BLOCK, SEL_TOPK)
    vbs = _to_blocks(vs_all, SEL_BLOCK, SEL_TOPK)
    overlap = _sel_overlap(kc.shape[1], kbs.shape[2])
    wb = win_buf.shape[1]
    pad = ((0, 0), (WINDOW, 0), (0, 0), (0, 0))
    kw_all = jnp.pad(jnp.concatenate([win_buf[:, :, 0], kw_n], axis=1), pad)
    vw_all = jnp.pad(jnp.concatenate([win_buf[:, :, 1], vw_n], axis=1), pad)

    def chunk(ci):
        r0 = ci * q_block
        q_pos = pos0 + r0 + jnp.arange(q_block)
        qm = lax.dynamic_slice_in_dim(q_m, r0, q_block, axis=1)
        qn = lax.dynamic_slice_in_dim(q_n, r0, q_block, axis=1)
        gt = lax.dynamic_slice_in_dim(gates, r0, q_block, axis=1)
        kw = lax.dynamic_slice_in_dim(kw_all, r0 + wb, WINDOW + q_block, axis=1)
        vw = lax.dynamic_slice_in_dim(vw_all, r0 + wb, WINDOW + q_block, axis=1)
        k_pos = pos0 + r0 - WINDOW + jnp.arange(WINDOW + q_block)
        o_m = _moba_attend(qm, q_pos, kb_m, vb_m, kmean, slopes_m)
        o_c, p_c = _nsa_cmp_attend(qn, q_pos, kc, vc, c_end, slopes_n)
        o_s = _nsa_sel_attend(qn, q_pos, p_c, overlap, kbs, vbs, slopes_n)
        o_w = _nsa_win_attend(qn, q_pos, kw, vw, k_pos, slopes_n)
        g = jax.nn.sigmoid(gt.astype(jnp.float32)).astype(qn.dtype).reshape(b, q_block, 3, NSA_HEADS, 1)
        o_n = g[:, :, 0] * o_c + g[:, :, 1] * o_s + g[:, :, 2] * o_w
        return jnp.concatenate([o_m.reshape(b, q_block, MOBA_W), o_n.reshape(b, q_block, NSA_W)], axis=-1)

    mixed = lax.map(chunk, jnp.arange(t // q_block))
    mixed = mixed.transpose(1, 0, 2, 3).reshape(b, t, MIX_W)
    x = x + ga_a * _rms_norm(mixed @ w_out, g_post_mix)
    h2 = _rms_norm(x, g_pre_mlp) * (1.0 + sc_m) + sh_m
    ff = jnp.square(jax.nn.relu(h2 @ w_up)) @ w_down
    y = x + ga_m * _rms_norm(ff, g_post_mlp)
    win_rows = jnp.stack([kw_all, vw_all], axis=2)[:, kw_all.shape[1] - win_out_len:]
    return y, jnp.stack([k_m, v_m], axis=2), jnp.stack([kc_n, vc_n], axis=2), jnp.stack([ks_n, vs_n], axis=2), win_rows


def setup_inputs(seed: int = 0) -> dict:
    key = jax.random.key(seed)
    ks = jax.random.split(key, 24)
    f32 = jnp.float32
    n_pages = PAST_LEN // PAGE_SIZE
    n_used = DEC_BATCH * n_pages
    n_phys = n_used + n_used // 4
    wb = min(WINDOW, PAST_LEN)

    def nrm(k, shape, s):
        return s * jax.random.normal(k, shape, f32)

    page_table = jax.random.permutation(ks[0], n_phys)[:n_used].reshape(DEC_BATCH, n_pages).astype(jnp.int32)
    return {
        'x_prompt': nrm(ks[1], (BATCH, SEQ, D_MODEL), 1.0),
        'x_sample': nrm(ks[2], (DEC_BATCH, DEC_SEQ, D_MODEL), 1.0),
        'cache_moba_kv': nrm(ks[3], (n_phys, PAGE_SIZE, 2, MOBA_HEADS, HEAD_DIM), 1.0),
        'cache_nsa_cmp_kv': nrm(ks[4], (n_phys, PAGE_SIZE, 2, NSA_KV_HEADS, HEAD_DIM), 1.0),
        'cache_nsa_sel_kv': nrm(ks[5], (n_phys, PAGE_SIZE, 2, NSA_KV_HEADS, HEAD_DIM), 1.0),
        'state_nsa_win_kv': nrm(ks[6], (DEC_BATCH, wb, 2, NSA_KV_HEADS, HEAD_DIM), 1.0),
        'page_table': page_table,
        'c_prompt': nrm(ks[7], (BATCH, D_MODEL), 1.0),
        'c_sample': nrm(ks[8], (DEC_BATCH, D_MODEL), 1.0),
        'w_ada': nrm(ks[9], (D_MODEL, 6 * D_MODEL), 0.5 * D_MODEL ** -0.5),
        'b_ada': nrm(ks[10], (6 * D_MODEL,), 0.01),
        'g_pre_mix': 1.0 + nrm(ks[11], (D_MODEL,), 0.05),
        'g_post_mix': 1.0 + nrm(ks[12], (D_MODEL,), 0.05),
        'g_pre_mlp': 1.0 + nrm(ks[13], (D_MODEL,), 0.05),
        'g_post_mlp': 1.0 + nrm(ks[14], (D_MODEL,), 0.05),
        'w_in': nrm(ks[15], (D_MODEL, PROJ_W), D_MODEL ** -0.5),
        'cmp_pos': nrm(ks[16], (2, CMP_LEN, HEAD_DIM), 0.1),
        'cmp_w1': nrm(ks[17], (2, CMP_LEN, HEAD_DIM, CMP_HIDDEN), (CMP_LEN * HEAD_DIM) ** -0.5),
        'cmp_b1': nrm(ks[18], (2, CMP_HIDDEN), 0.01),
        'cmp_w2': nrm(ks[19], (2, CMP_HIDDEN, HEAD_DIM), CMP_HIDDEN ** -0.5),
        'w_out': nrm(ks[20], (MIX_W, D_MODEL), MIX_W ** -0.5),
        'w_up': nrm(ks[21], (D_MODEL, D_FF), D_MODEL ** -0.5),
        'w_down': nrm(ks[22], (D_FF, D_MODEL), D_FF ** -0.5),
    }


def reference(x_prompt, x_sample, cache_moba_kv, cache_nsa_cmp_kv, cache_nsa_sel_kv, state_nsa_win_kv,
              page_table, c_prompt, c_sample, w_ada, b_ada, g_pre_mix, g_post_mix, g_pre_mlp, g_post_mlp,
              w_in, cmp_pos, cmp_w1, cmp_b1, cmp_w2, w_out, w_up, w_down):
    b, dt = x_prompt.shape[0], x_prompt.dtype
    empty_m = jnp.zeros((b, 0, 2, MOBA_HEADS, HEAD_DIM), dt)
    empty_n = jnp.zeros((b, 0, 2, NSA_KV_HEADS, HEAD_DIM), dt)
    past_moba = _gather_pages(cache_moba_kv, page_table)
    past_cmp = _gather_pages(cache_nsa_cmp_kv, page_table)
    past_sel = _gather_pages(cache_nsa_sel_kv, page_table)
    past_len = past_moba.shape[1]
    y_p, y_s = x_prompt, x_sample
    for _ in range(DEPTH):
        y_p, moba_p, cmp_p, sel_p, win_p = _layer(
            y_p, c_prompt, empty_m, empty_n, empty_n, empty_n, 0, Q_BLOCK, min(WINDOW, y_p.shape[1]),
            w_ada, b_ada, g_pre_mix, g_post_mix, g_pre_mlp, g_post_mlp, w_in,
            cmp_pos, cmp_w1, cmp_b1, cmp_w2, w_out, w_up, w_down)
        y_s, moba_s, cmp_s, sel_s, win_s = _layer(
            y_s, c_sample, past_moba, past_cmp, past_sel, state_nsa_win_kv, past_len, 1, state_nsa_win_kv.shape[1],
            w_ada, b_ada, g_pre_mix, g_post_mix, g_pre_mlp, g_post_mlp, w_in,
            cmp_pos, cmp_w1, cmp_b1, cmp_w2, w_out, w_up, w_down)
    return (y_p, y_s, moba_p, moba_s, cmp_p, cmp_s, sel_p, sel_s, win_p, win_s)
```

```python
import numpy as np
import jax
import jax.numpy as jnp
from jax import lax
from jax.experimental import pallas as pl
from jax.experimental.pallas import tpu as pltpu

HEAD_DIM = 64
MOBA_HEADS = 8
NSA_HEADS = 8
NSA_KV_HEADS = 2
NSA_GROUP = NSA_HEADS // NSA_KV_HEADS
MOBA_BLOCK = 256
MOBA_TOPK = 3
CMP_LEN = 32
CMP_STRIDE = 16
SEL_BLOCK = 64
SEL_TOPK = 16
WINDOW = 512
Q_BLOCK = 64
EPS = 1e-6
SEL_FORCE = 1e9
MASK_NEG = -1e30
NEG = -1e30
MOBA_W = MOBA_HEADS * HEAD_DIM
NSA_W = NSA_HEADS * HEAD_DIM
KV_W = NSA_KV_HEADS * HEAD_DIM
N_GATES = 3 * NSA_HEADS
PROJ_W = 3 * MOBA_W + NSA_W + 6 * KV_W + N_GATES

LANES = 128
VMEM_LIMIT = 48 * 1024 * 1024
PROJ_PAD = -(-PROJ_W // LANES) * LANES
COL_QM = 0
COL_KM = MOBA_W // LANES
F32 = jnp.float32
BF16 = jnp.bfloat16


def _alibi_slopes():
    n = MOBA_HEADS + NSA_HEADS
    m = 2.0 ** (-8.0 * jnp.arange(1, n + 1, dtype=jnp.float32) / n)
    return m[0::2][:MOBA_HEADS], m[1::2][:NSA_HEADS]


def _ada_kernel(c_ref, w_ref, b_ref, o_ref):
    c = c_ref[...]
    a = (c * jax.nn.sigmoid(c)).astype(BF16)
    o_ref[...] = jnp.dot(a, w_ref[...].astype(BF16), preferred_element_type=F32) + b_ref[...]


def _ada(c, w_ada, b_ada, tn=768):
    rows, d = c.shape
    n = w_ada.shape[1]
    return pl.pallas_call(
        _ada_kernel,
        out_shape=jax.ShapeDtypeStruct((rows, n), F32),
        grid=(n // tn,),
        in_specs=[pl.BlockSpec((rows, d), lambda j: (0, 0)),
                  pl.BlockSpec((d, tn), lambda j: (0, j)),
                  pl.BlockSpec((1, tn), lambda j: (0, j))],
        out_specs=pl.BlockSpec((rows, tn), lambda j: (0, j)),
        compiler_params=pltpu.CompilerParams(dimension_semantics=("arbitrary",), vmem_limit_bytes=VMEM_LIMIT),
        name="ada_mod",
    )(c, w_ada, b_ada.reshape(1, n))


def _prep_w_in(w_in):
    scale = np.ones((w_in.shape[1],), np.float32)
    scale[:MOBA_W] = HEAD_DIM ** -0.5
    scale[3 * MOBA_W:3 * MOBA_W + NSA_W] = HEAD_DIM ** -0.5
    w = w_in * scale
    return jnp.pad(w, ((0, 0), (0, PROJ_PAD - w.shape[1]))).astype(BF16)


def _proj_kernel(x_ref, g_ref, sc_ref, sh_ref, w_ref, o_ref):
    x = x_ref[0]
    y = x * lax.rsqrt(jnp.mean(x * x, axis=-1, keepdims=True) + EPS) * g_ref[...]
    h = y * (1.0 + sc_ref[0]) + sh_ref[0]
    o_ref[0] = jnp.dot(h.astype(BF16), w_ref[...], preferred_element_type=F32)


def _proj(x, g, sc, sh, w_bf16, tm):
    grp, rows, d = x.shape
    rm = sc.shape[1]
    mod_rows = tm if rm == rows else 1
    mod_map = (lambda b, i: (b, i, 0)) if rm == rows else (lambda b, i: (b, 0, 0))
    npad = w_bf16.shape[1]
    return pl.pallas_call(
        _proj_kernel,
        out_shape=jax.ShapeDtypeStruct((grp, rows, npad), F32),
        grid=(grp, rows // tm),
        in_specs=[pl.BlockSpec((1, tm, d), lambda b, i: (b, i, 0)),
                  pl.BlockSpec((1, d), lambda b, i: (0, 0)),
                  pl.BlockSpec((1, mod_rows, d), mod_map),
                  pl.BlockSpec((1, mod_rows, d), mod_map),
                  pl.BlockSpec((d, npad), lambda b, i: (0, 0))],
        out_specs=pl.BlockSpec((1, tm, npad), lambda b, i: (b, i, 0)),
        compiler_params=pltpu.CompilerParams(dimension_semantics=("arbitrary", "arbitrary"),
                                             vmem_limit_bytes=VMEM_LIMIT),
        name="in_proj",
    )(x, g.reshape(1, d), sc, sh, w_bf16)


def _flash_step(s_t, vt_tile, row_ok, row_shift, m_ref, l_ref, acc_ref):
    m_old = m_ref[...]
    cmax = jnp.max(s_t, axis=0, keepdims=True) - row_shift
    m_new = jnp.maximum(m_old, jnp.where(row_ok, cmax, NEG))
    shift = jnp.where(row_ok, m_new + row_shift, -NEG)
    p = jnp.exp(s_t - shift)
    alpha = jnp.exp(m_old - m_new)
    l_ref[...] = alpha * l_ref[...] + jnp.sum(p, axis=0, keepdims=True)
    acc_ref[...] = alpha * acc_ref[...] + jnp.dot(vt_tile, p.astype(BF16), preferred_element_type=F32)
    m_ref[...] = m_new


def _stack_heads_t(q2):
    qt = q2.T
    low = lax.broadcasted_iota(jnp.int32, qt.shape, 0) < HEAD_DIM
    return jnp.concatenate([jnp.where(low, qt, 0.0), jnp.where(low, 0.0, qt)], axis=1)


def _unstack_heads(o_t, tq):
    o2t = jnp.concatenate([o_t[:HEAD_DIM, :tq], o_t[HEAD_DIM:, tq:]], axis=0)
    return o2t.T


def _split_bf16(a):
    hi = a.astype(BF16)
    return hi, (a - hi.astype(F32)).astype(BF16)


def _dot_f32ish(a, b):
    a_hi, a_lo = _split_bf16(a)
    b_hi, b_lo = _split_bf16(b)
    return (jnp.dot(a_hi, b_hi, preferred_element_type=F32) + jnp.dot(a_hi, b_lo, preferred_element_type=F32)
            + jnp.dot(a_lo, b_hi, preferred_element_type=F32))


def _moba_prompt_kernel(slope_ref, q_ref, km_ref, kb_ref, vt_ref, o_ref, m_ref, l_ref, acc_ref, sel_ref, d0_ref):
    j = pl.program_id(1)
    i = pl.program_id(2)
    tq = q_ref.shape[1]
    nb = km_ref.shape[1]
    m2 = 2 * tq
    qs_t = _stack_heads_t(q_ref[0])
    lane = lax.broadcasted_iota(jnp.int32, (1, m2), 1)
    slope = jnp.where(lane < tq, slope_ref[2 * j], slope_ref[2 * j + 1])
    gate = _dot_f32ish(km_ref[0], qs_t)
    blk = lax.broadcasted_iota(jnp.int32, (nb, m2), 0)
    gate = jnp.where(blk < i, gate, -jnp.inf)
    sel = jnp.zeros((nb, m2), F32)
    for _ in range(MOBA_TOPK):
        top = jnp.max(gate, axis=0, keepdims=True)
        idx = jnp.min(jnp.where(gate == top, blk, nb), axis=0, keepdims=True)
        hit = blk == idx
        sel = jnp.where(jnp.logical_and(hit, top > -jnp.inf), 1.0, sel)
        gate = jnp.where(hit, -jnp.inf, gate)
    sel_ref[...] = sel
    r_q = jnp.where(lane < tq, lane, lane - tq)
    c_k = lax.broadcasted_iota(jnp.int32, (MOBA_BLOCK, m2), 0)
    rel = r_q - c_k
    d0_ref[...] = slope * rel.astype(F32)
    m_ref[...] = jnp.full(m_ref.shape, NEG, F32)
    l_ref[...] = jnp.zeros(l_ref.shape, F32)
    acc_ref[...] = jnp.zeros(acc_ref.shape, F32)
    qs_b = qs_t.astype(BF16)

    def past(n, carry):
        s_t = jnp.dot(kb_ref[0, n], qs_b, preferred_element_type=F32) - d0_ref[...]
        row_ok = sel_ref[pl.ds(n, 1), :] > 0.5
        off = ((i - n) * MOBA_BLOCK).astype(F32)
        _flash_step(s_t, vt_ref[0, 0, n], row_ok, slope * off, m_ref, l_ref, acc_ref)
        return carry

    lax.fori_loop(0, i, past, 0)
    s_t = jnp.dot(kb_ref[0, i], qs_b, preferred_element_type=F32) - d0_ref[...]
    s_t = jnp.where(rel >= 0, s_t, NEG)
    _flash_step(s_t, vt_ref[0, 0, i], lane >= 0, jnp.zeros((1, m2), F32), m_ref, l_ref, acc_ref)
    o_ref[0] = _unstack_heads(acc_ref[...] / l_ref[...], tq)


def _moba_prompt(q, km, kb, vt, slopes):
    b, t, _ = q.shape
    tq = MOBA_BLOCK
    nb = km.shape[1]
    npair = MOBA_W // LANES
    return pl.pallas_call(
        _moba_prompt_kernel,
        out_shape=jax.ShapeDtypeStruct((b, t, MOBA_W), F32),
        grid=(b, npair, t // tq),
        in_specs=[pl.BlockSpec(memory_space=pltpu.SMEM),
                  pl.BlockSpec((1, tq, LANES), lambda bb, j, i: (bb, i, j)),
                  pl.BlockSpec((1, nb, LANES), lambda bb, j, i: (bb, 0, j)),
                  pl.BlockSpec((1, nb, MOBA_BLOCK, LANES), lambda bb, j, i: (bb, 0, 0, j)),
                  pl.BlockSpec((1, 1, nb, LANES, MOBA_BLOCK), lambda bb, j, i: (bb, j, 0, 0, 0))],
        out_specs=pl.BlockSpec((1, tq, LANES), lambda bb, j, i: (bb, i, j)),
        scratch_shapes=[pltpu.VMEM((1, 2 * tq), F32), pltpu.VMEM((1, 2 * tq), F32),
                        pltpu.VMEM((LANES, 2 * tq), F32), pltpu.VMEM((nb, 2 * tq), F32),
                        pltpu.VMEM((MOBA_BLOCK, 2 * tq), F32)],
        compiler_params=pltpu.CompilerParams(dimension_semantics=("arbitrary",) * 3, vmem_limit_bytes=VMEM_LIMIT),
        name="moba_prompt",
    )(slopes, q, km, kb, vt)


def _rms_norm(x, g):
    xf = x.astype(jnp.float32)
    y = xf * lax.rsqrt(jnp.mean(xf * xf, axis=-1, keepdims=True) + EPS)
    return (y * g.astype(jnp.float32)).astype(x.dtype)


def _to_blocks(k, block, min_blocks):
    b, l, h, d = k.shape
    nb = max(-(-l // block), min_blocks)
    k = jnp.pad(k, ((0, 0), (0, nb * block - l), (0, 0), (0, 0)))
    return k.reshape(b, nb, block, h, d).transpose(0, 3, 1, 2, 4)


def _gather_blocks(blocks, idx):
    return jax.vmap(jax.vmap(lambda a, i: a[i]))(blocks, idx)


def _gather_pages(pool, page_table):
    g = pool[page_table]
    return g.reshape((g.shape[0], g.shape[1] * g.shape[2]) + pool.shape[2:])


def _compress(k, pos, w1, b1, w2):
    b, l, h, d = k.shape
    lp = max(l, CMP_LEN)
    k = jnp.pad(k, ((0, 0), (0, lp - l), (0, 0), (0, 0)))
    nc = (lp - CMP_LEN) // CMP_STRIDE + 1
    idx = jnp.arange(nc)[:, None] * CMP_STRIDE + jnp.arange(CMP_LEN)[None, :]
    win = k[:, idx] + pos[None, None, :, None, :]
    hid = jax.nn.gelu(jnp.einsum('bnlhd,ldf->bnhf', win, w1) + b1)
    return jnp.einsum('bnhf,fd->bnhd', hid, w2)


def _sel_overlap(nc, ns):
    start = jnp.arange(nc)[:, None] * CMP_STRIDE
    blk = jnp.arange(ns)[None, :] * SEL_BLOCK
    return ((start <= blk + SEL_BLOCK - 1) & (start + CMP_LEN - 1 >= blk)).astype(jnp.float32)


def _moba_attend(q, q_pos, kb, vb, kmean, slopes):
    b, tq, h, d = q.shape
    nb, blk = kb.shape[2], kb.shape[3]
    qt = q.transpose(0, 2, 1, 3)
    gate = jnp.einsum('bhtd,bhnd->bhtn', qt.astype(jnp.float32), kmean)
    own = q_pos // MOBA_BLOCK
    past = jnp.arange(nb)[None, :] < own[:, None]
    gate = jnp.where(past, gate, -jnp.inf)
    top_val, top_idx = lax.top_k(gate, MOBA_TOPK)
    idx = jnp.concatenate([top_idx.astype(jnp.int32),
                           jnp.broadcast_to(own[None, None, :, None], (b, h, tq, 1)).astype(jnp.int32)], axis=-1)
    ok = jnp.concatenate([jnp.isfinite(top_val), jnp.ones((b, h, tq, 1), bool)], axis=-1)
    kg = _gather_blocks(kb, idx)
    vg = _gather_blocks(vb, idx)
    kpos = idx[..., None] * blk + jnp.arange(blk)
    dist = (q_pos[None, None, :, None, None] - kpos).astype(jnp.float32)
    mask = ok[..., None] & (dist >= 0)
    s = jnp.einsum('bhtd,bhtjmd->bhtjm', qt, kg, preferred_element_type=jnp.float32) * (d ** -0.5)
    s = s - slopes[None, :, None, None, None] * dist
    s = jnp.where(mask, s, -jnp.inf).reshape(b, h, tq, -1)
    p = jax.nn.softmax(s, axis=-1).astype(vg.dtype)
    return jnp.einsum('bhtn,bhtnd->bthd', p, vg.reshape(b, h, tq, -1, d))


def _nsa_cmp_attend(q, q_pos, kc, vc, c_end, slopes):
    b, tq, h, d = q.shape
    qg = q.reshape(b, tq, NSA_KV_HEADS, NSA_GROUP, d)
    sg = slopes.reshape(NSA_KV_HEADS, NSA_GROUP)
    dist = (q_pos[:, None] - c_end[None, :]).astype(jnp.float32)
    mask = (dist >= 0)[None, :, None, None, :]
    s = jnp.einsum('btkgd,bnkd->btkgn', qg, kc, preferred_element_type=jnp.float32) * (d ** -0.5)
    s = s - sg[None, None, :, :, None] * dist[None, :, None, None, :]
    s = jnp.where(mask, s, MASK_NEG)
    p = jax.nn.softmax(s, axis=-1) * mask
    o = jnp.einsum('btkgn,bnkd->btkgd', p.astype(vc.dtype), vc).reshape(b, tq, h, d)
    return o, p


def _nsa_sel_attend(q, q_pos, p_cmp, overlap, kbs, vbs, slopes):
    b, tq, h, d = q.shape
    ns, blk = kbs.shape[2], kbs.shape[3]
    score = jnp.einsum('btkgn,ns->bkts', p_cmp, overlap)
    j = jnp.arange(ns)[None, :]
    own = (q_pos // SEL_BLOCK)[:, None]
    forced = (j == 0) | (j == own) | (j == own - 1)
    score = jnp.where(forced, SEL_FORCE, score)
    score = jnp.where(j <= own, score, -jnp.inf)
    top_val, idx = lax.top_k(score, SEL_TOPK)
    idx = idx.astype(jnp.int32)
    ok = jnp.isfinite(top_val)
    kg = _gather_blocks(kbs, idx)
    vg = _gather_blocks(vbs, idx)
    kpos = idx[..., None] * blk + jnp.arange(blk)
    dist = (q_pos[None, None, :, None, None] - kpos).astype(jnp.float32)
    mask = ok[..., None] & (dist >= 0)
    qg = q.reshape(b, tq, NSA_KV_HEADS, NSA_GROUP, d).transpose(0, 2, 1, 3, 4)
    sg = slopes.reshape(NSA_KV_HEADS, NSA_GROUP)
    s = jnp.einsum('bktgd,bktjmd->bktgjm', qg, kg, preferred_element_type=jnp.float32) * (d ** -0.5)
    s = s - sg[None, :, None, :, None, None] * dist[:, :, :, None]
    s = jnp.where(mask[:, :, :, None], s, -jnp.inf).reshape(b, NSA_KV_HEADS, tq, NSA_GROUP, -1)
    p = jax.nn.softmax(s, axis=-1).astype(vg.dtype)
    o = jnp.einsum('bktgn,bktnd->btkgd', p, vg.reshape(b, NSA_KV_HEADS, tq, -1, d))
    return o.reshape(b, tq, h, d)


def _nsa_win_attend(q, q_pos, kw, vw, k_pos, slopes):
    b, tq, h, d = q.shape
    qg = q.reshape(b, tq, NSA_KV_HEADS, NSA_GROUP, d)
    sg = slopes.reshape(NSA_KV_HEADS, NSA_GROUP)
    dist = (q_pos[:, None] - k_pos[None, :]).astype(jnp.float32)
    mask = (dist >= 0) & (dist < WINDOW) & (k_pos >= 0)[None, :]
    s = jnp.einsum('btkgd,bskd->btkgs', qg, kw, preferred_element_type=jnp.float32) * (d ** -0.5)
    s = s - sg[None, None, :, :, None] * dist[None, :, None, None, :]
    s = jnp.where(mask[None, :, None, None, :], s, -jnp.inf)
    p = jax.nn.softmax(s, axis=-1).astype(vw.dtype)
    return jnp.einsum('btkgs,bskd->btkgd', p, vw).reshape(b, tq, h, d)


def _layer(x, mod, past_moba, past_cmp, past_sel, win_buf, pos0, q_block, win_out_len, pallas_moba,
           g_pre_mix, g_post_mix, g_pre_mlp, g_post_mlp, w_in_prep,
           cmp_pos, cmp_w1, cmp_b1, cmp_w2, w_out, w_up, w_down):
    b, t, d_model = x.shape
    slopes_m, slopes_n = _alibi_slopes()
    sh_a, sc_a, ga_a, sh_m, sc_m, ga_m = jnp.split(mod[:, None, :], 6, axis=-1)
    if pallas_moba:
        projp = _proj(x, g_pre_mix, sc_a, sh_a, w_in_prep, 512)
    else:
        rows = b * t
        rep = lambda a: jnp.broadcast_to(a, (b, t, d_model)).reshape(1, rows, d_model)
        projp = _proj(x.reshape(1, rows, d_model), g_pre_mix, rep(sc_a), rep(sh_a), w_in_prep, 256)
        projp = projp.reshape(b, t, PROJ_PAD)
    unscale = np.ones((PROJ_W,), np.float32)
    unscale[:MOBA_W] = HEAD_DIM ** 0.5
    unscale[3 * MOBA_W:3 * MOBA_W + NSA_W] = HEAD_DIM ** 0.5
    proj = projp[:, :, :PROJ_W] * unscale
    sizes = [MOBA_W] * 3 + [NSA_W] + [KV_W] * 6 + [N_GATES]
    cuts = [int(v) for v in np.cumsum(sizes)[:-1]]
    q_m, k_m, v_m, q_n, kc_n, vc_n, ks_n, vs_n, kw_n, vw_n, gates = jnp.split(proj, cuts, axis=-1)
    k_m_flat, v_m_flat = k_m, v_m
    q_m, k_m, v_m = [a.reshape(b, t, MOBA_HEADS, HEAD_DIM) for a in (q_m, k_m, v_m)]
    q_n = q_n.reshape(b, t, NSA_HEADS, HEAD_DIM)
    kc_n, vc_n, ks_n, vs_n, kw_n, vw_n = [a.reshape(b, t, NSA_KV_HEADS, HEAD_DIM)
                                          for a in (kc_n, vc_n, ks_n, vs_n, kw_n, vw_n)]
    kc_all = jnp.concatenate([past_cmp[:, :, 0], kc_n], axis=1)
    vc_all = jnp.concatenate([past_cmp[:, :, 1], vc_n], axis=1)
    ks_all = jnp.concatenate([past_sel[:, :, 0], ks_n], axis=1)
    vs_all = jnp.concatenate([past_sel[:, :, 1], vs_n], axis=1)
    if pallas_moba:
        nblk = t // MOBA_BLOCK
        npair = MOBA_W // LANES
        km = jnp.mean(k_m_flat.reshape(b, nblk, MOBA_BLOCK, MOBA_W), axis=2)
        kb = k_m_flat.astype(BF16).reshape(b, nblk, MOBA_BLOCK, MOBA_W)
        vt = v_m_flat.astype(BF16).reshape(b, nblk, MOBA_BLOCK, npair, LANES).transpose(0, 3, 1, 4, 2)
        o_m_all = _moba_prompt(projp[:, :, :MOBA_W], km, kb, vt, slopes_m)
    else:
        k_m_all = jnp.concatenate([past_moba[:, :, 0], k_m], axis=1)
        v_m_all = jnp.concatenate([past_moba[:, :, 1], v_m], axis=1)
        kb_m = _to_blocks(k_m_all, MOBA_BLOCK, MOBA_TOPK)
        vb_m = _to_blocks(v_m_all, MOBA_BLOCK, MOBA_TOPK)
        kmean = jnp.mean(kb_m.astype(jnp.float32), axis=3)
    kc = _compress(kc_all, cmp_pos[0], cmp_w1[0], cmp_b1[0], cmp_w2[0])
    vc = _compress(vc_all, cmp_pos[1], cmp_w1[1], cmp_b1[1], cmp_w2[1])
    c_end = jnp.arange(kc.shape[1]) * CMP_STRIDE + CMP_LEN - 1
    kbs = _to_blocks(ks_all, SEL_BLOCK, SEL_TOPK)
    vbs = _to_blocks(vs_all, SEL_BLOCK, SEL_TOPK)
    overlap = _sel_overlap(kc.shape[1], kbs.shape[2])
    wb = win_buf.shape[1]
    pad = ((0, 0), (WINDOW, 0), (0, 0), (0, 0))
    kw_all = jnp.pad(jnp.concatenate([win_buf[:, :, 0], kw_n], axis=1), pad)
    vw_all = jnp.pad(jnp.concatenate([win_buf[:, :, 1], vw_n], axis=1), pad)

    def chunk(ci):
        r0 = ci * q_block
        q_pos = pos0 + r0 + jnp.arange(q_block)
        qn = lax.dynamic_slice_in_dim(q_n, r0, q_block, axis=1)
        gt = lax.dynamic_slice_in_dim(gates, r0, q_block, axis=1)
        kw = lax.dynamic_slice_in_dim(kw_all, r0 + wb, WINDOW + q_block, axis=1)
        vw = lax.dynamic_slice_in_dim(vw_all, r0 + wb, WINDOW + q_block, axis=1)
        k_pos = pos0 + r0 - WINDOW + jnp.arange(WINDOW + q_block)
        if pallas_moba:
            o_m = lax.dynamic_slice_in_dim(o_m_all, r0, q_block, axis=1)
        else:
            qm = lax.dynamic_slice_in_dim(q_m, r0, q_block, axis=1)
            o_m = _moba_attend(qm, q_pos, kb_m, vb_m, kmean, slopes_m)
        o_c, p_c = _nsa_cmp_attend(qn, q_pos, kc, vc, c_end, slopes_n)
        o_s = _nsa_sel_attend(qn, q_pos, p_c, overlap, kbs, vbs, slopes_n)
        o_w = _nsa_win_attend(qn, q_pos, kw, vw, k_pos, slopes_n)
        g = jax.nn.sigmoid(gt.astype(jnp.float32)).astype(qn.dtype).reshape(b, q_block, 3, NSA_HEADS, 1)
        o_n = g[:, :, 0] * o_c + g[:, :, 1] * o_s + g[:, :, 2] * o_w
        return jnp.concatenate([o_m.reshape(b, q_block, MOBA_W), o_n.reshape(b, q_block, NSA_W)], axis=-1)

    mixed = lax.map(chunk, jnp.arange(t // q_block))
    mixed = mixed.transpose(1, 0, 2, 3).reshape(b, t, MOBA_W + NSA_W)
    x = x + ga_a * _rms_norm(mixed @ w_out, g_post_mix)
    h2 = _rms_norm(x, g_pre_mlp) * (1.0 + sc_m) + sh_m
    ff = jnp.square(jax.nn.relu(h2 @ w_up)) @ w_down
    y = x + ga_m * _rms_norm(ff, g_post_mlp)
    win_rows = jnp.stack([kw_all, vw_all], axis=2)[:, kw_all.shape[1] - win_out_len:]
    return (y, jnp.stack([k_m, v_m], axis=2), jnp.stack([kc_n, vc_n], axis=2),
            jnp.stack([ks_n, vs_n], axis=2), win_rows)


def kernel(x_prompt, x_sample, cache_moba_kv, cache_nsa_cmp_kv, cache_nsa_sel_kv, state_nsa_win_kv,
           page_table, c_prompt, c_sample, w_ada, b_ada, g_pre_mix, g_post_mix, g_pre_mlp, g_post_mlp,
           w_in, cmp_pos, cmp_w1, cmp_b1, cmp_w2, w_out, w_up, w_down):
    b, dt = x_prompt.shape[0], x_prompt.dtype
    nb_s = x_sample.shape[0]
    empty_m = jnp.zeros((b, 0, 2, MOBA_HEADS, HEAD_DIM), dt)
    empty_n = jnp.zeros((b, 0, 2, NSA_KV_HEADS, HEAD_DIM), dt)
    past_moba = _gather_pages(cache_moba_kv, page_table)
    past_cmp = _gather_pages(cache_nsa_cmp_kv, page_table)
    past_sel = _gather_pages(cache_nsa_sel_kv, page_table)
    past_len = past_moba.shape[1]
    c_all = jnp.concatenate([c_prompt, c_sample], axis=0)
    pad_rows = -c_all.shape[0] % 8
    mod_all = _ada(jnp.pad(c_all, ((0, pad_rows), (0, 0))), w_ada, b_ada)
    mod_p, mod_s = mod_all[:b], mod_all[b:b + nb_s]
    w_in_prep = _prep_w_in(w_in)
    weights = (g_pre_mix, g_post_mix, g_pre_mlp, g_post_mlp, w_in_prep,
               cmp_pos, cmp_w1, cmp_b1, cmp_w2, w_out, w_up, w_down)
    t_p = x_prompt.shape[1]
    y_p, moba_p, cmp_p, sel_p, win_p = _layer(
        x_prompt, mod_p, empty_m, empty_n, empty_n, empty_n, 0, Q_BLOCK, min(WINDOW, t_p),
        t_p % 512 == 0, *weights)
    y_s, moba_s, cmp_s, sel_s, win_s = _layer(
        x_sample, mod_s, past_moba, past_cmp, past_sel, state_nsa_win_kv, past_len, 1,
        state_nsa_win_kv.shape[1], False, *weights)
    return (y_p, y_s, moba_p, moba_s, cmp_p, cmp_s, sel_p, sel_s, win_p, win_s)
```
